```python
import math
import jax, jax.numpy as jnp
from jax import lax
import numpy as np

D_MODEL = 1024
BATCH = 2
SEQ = 8192
DEPTH = 1

PLE_DIM = 256
D_FF = 2816
CONV_WIDTH = D_MODEL // 2
CONV_K = 3
ATTN_WIDTH = D_MODEL // 2
HEAD_DIM = 64
N_DIFF_HEADS = ATTN_WIDTH // (2 * HEAD_DIM)
V_HEAD_DIM = 2 * HEAD_DIM
MIX_WIDTH = CONV_WIDTH + ATTN_WIDTH
IN_PROJ_WIDTH = 3 * CONV_WIDTH + 3 * ATTN_WIDTH
Q_BLOCK = 128
ROPE_THETA = 10000.0
EPS = 1e-6
LAMBDA_STD = 0.1

kernel_name = "hymba_conv_diffattn_macaron_ple"


def rmsnorm(x, g):
    xf = x.astype(jnp.float32)
    xf = xf * lax.rsqrt(jnp.mean(xf * xf, axis=-1, keepdims=True) + EPS)
    return xf.astype(x.dtype) * g


def swiglu(x, w_gate, w_up, w_down):
    return (jax.nn.silu(x @ w_gate) * (x @ w_up)) @ w_down


def causal_conv3(u, w):
    s = u.shape[1]
    up = jnp.pad(u, ((0, 0), (CONV_K - 1, 0), (0, 0)))
    return w[0] * up[:, 0:s] + w[1] * up[:, 1:s + 1] + w[2] * up[:, 2:s + 2]


def rope_tables(s):
    inv_freq = 1.0 / (ROPE_THETA ** (jnp.arange(0, HEAD_DIM, 2, dtype=jnp.float32) / HEAD_DIM))
    ang = jnp.arange(s, dtype=jnp.float32)[:, None] * inv_freq[None, :]
    return jnp.cos(ang), jnp.sin(ang)


def apply_rope(x, cos, sin):
    c = cos[:, None, None, :].astype(x.dtype)
    sn = sin[:, None, None, :].astype(x.dtype)
    x1, x2 = jnp.split(x, 2, axis=-1)
    return jnp.concatenate([x1 * c - x2 * sn, x2 * c + x1 * sn], axis=-1)


def diff_attention(q, k, v, lam):
    b, s = q.shape[0], q.shape[1]
    nblk = s // Q_BLOCK
    scale = HEAD_DIM ** -0.5
    qb = q.reshape(b, nblk, Q_BLOCK, N_DIFF_HEADS, 2, HEAD_DIM).transpose(1, 0, 2, 3, 4, 5)
    kf = k.astype(jnp.float32)
    vf = v.astype(jnp.float32)
    kpos = jnp.arange(s)

    def one_block(args):
        qi, blk = args
        qpos = blk * Q_BLOCK + jnp.arange(Q_BLOCK)
        sc = jnp.einsum('bqhcd,bkhcd->bhcqk', qi.astype(jnp.float32), kf) * scale
        mask = kpos[None, :] <= qpos[:, None]
        sc = jnp.where(mask, sc, -jnp.inf)
        pr = jax.nn.softmax(sc, axis=-1)
        a = pr[:, :, 0] - lam * pr[:, :, 1]
        return jnp.einsum('bhqk,bkhe->bqhe', a, vf).astype(v.dtype)

    out = lax.map(one_block, (qb, jnp.arange(nblk)))
    return out.transpose(1, 0, 2, 3, 4).reshape(b, s, N_DIFF_HEADS, V_HEAD_DIM)


def setup_inputs(seed: int = 0) -> dict:
    key = jax.random.key(seed)
    ks = jax.random.split(key, 24)

    def w(k, shape, fan_in):
        return jax.random.normal(k, shape, jnp.float32) * (fan_in ** -0.5)

    def gain(k, shape):
        return 1.0 + 0.01 * jax.random.normal(k, shape, jnp.float32)

    L = DEPTH
    return {
        "x": jax.random.normal(ks[0], (BATCH, SEQ, D_MODEL), jnp.float32),
        "p": jax.random.normal(ks[1], (DEPTH, BATCH, SEQ, PLE_DIM), jnp.float32),
        "ffn1_norm": gain(ks[2], (L, D_MODEL)),
        "ffn1_w_gate": w(ks[3], (L, D_MODEL, D_FF), D_MODEL),
        "ffn1_w_up": w(ks[4], (L, D_MODEL, D_FF), D_MODEL),
        "ffn1_w_down": w(ks[5], (L, D_FF, D_MODEL), D_FF),
        "mix_norm": gain(ks[6], (L, D_MODEL)),
        "w_in": w(ks[7], (L, D_MODEL, IN_PROJ_WIDTH), D_MODEL),
        "conv_w": w(ks[8], (L, CONV_K, CONV_WIDTH), CONV_K),
        "lambda_q1": LAMBDA_STD * jax.random.normal(ks[9], (L, HEAD_DIM), jnp.float32),
        "lambda_k1": LAMBDA_STD * jax.random.normal(ks[10], (L, HEAD_DIM), jnp.float32),
        "lambda_q2": LAMBDA_STD * jax.random.normal(ks[11], (L, HEAD_DIM), jnp.float32),
        "lambda_k2": LAMBDA_STD * jax.random.normal(ks[12], (L, HEAD_DIM), jnp.float32),
        "subln_g": gain(ks[13], (L, V_HEAD_DIM)),
        "w_out": w(ks[14], (L, MIX_WIDTH, D_MODEL), MIX_WIDTH),
        "ffn2_norm": gain(ks[15], (L, D_MODEL)),
        "ffn2_w_gate": w(ks[16], (L, D_MODEL, D_FF), D_MODEL),
        "ffn2_w_up": w(ks[17], (L, D_MODEL, D_FF), D_MODEL),
        "ffn2_w_down": w(ks[18], (L, D_FF, D_MODEL), D_FF),
        "ple_norm": gain(ks[19], (L, D_MODEL)),
        "w_ple_gate": w(ks[20], (L, D_MODEL, D_MODEL), D_MODEL),
        "w_ple_proj": w(ks[21], (L, PLE_DIM, D_MODEL), PLE_DIM),
        "final_norm": gain(ks[22], (D_MODEL,)),
    }


def reference(x, p, ffn1_norm, ffn1_w_gate, ffn1_w_up, ffn1_w_down, mix_norm, w_in, conv_w,
              lambda_q1, lambda_k1, lambda_q2, lambda_k2, subln_g, w_out,
              ffn2_norm, ffn2_w_gate, ffn2_w_up, ffn2_w_down,
              ple_norm, w_ple_gate, w_ple_proj, final_norm):
    b, s, _ = x.shape
    cos, sin = rope_tables(s)
    splits = [CONV_WIDTH, 2 * CONV_WIDTH, 3 * CONV_WIDTH,
              3 * CONV_WIDTH + ATTN_WIDTH, 3 * CONV_WIDTH + 2 * ATTN_WIDTH]
    h = x
    for i in range(DEPTH):
        lam_init = 0.8 - 0.6 * math.exp(-0.3 * i)

        h = h + 0.5 * swiglu(rmsnorm(h, ffn1_norm[i]), ffn1_w_gate[i], ffn1_w_up[i], ffn1_w_down[i])

        n = rmsnorm(h, mix_norm[i])
        proj = n @ w_in[i]
        c_b, c_c, c_x, q, k, v = jnp.split(proj, splits, axis=-1)

        y_conv = c_b * causal_conv3(c_c * c_x, conv_w[i])

        q = apply_rope(q.reshape(b, s, N_DIFF_HEADS, 2, HEAD_DIM), cos, sin)
        k = apply_rope(k.reshape(b, s, N_DIFF_HEADS, 2, HEAD_DIM), cos, sin)
        v = v.reshape(b, s, N_DIFF_HEADS, V_HEAD_DIM)
        lam = (jnp.exp(jnp.sum(lambda_q1[i].astype(jnp.float32) * lambda_k1[i].astype(jnp.float32)))
               - jnp.exp(jnp.sum(lambda_q2[i].astype(jnp.float32) * lambda_k2[i].astype(jnp.float32)))
               + lam_init)
        o = diff_attention(q, k, v, lam)
        o = rmsnorm(o, subln_g[i]) * (1.0 - lam_init)
        y_attn = o.reshape(b, s, ATTN_WIDTH)

        h = h + jnp.concatenate([y_conv, y_attn], axis=-1) @ w_out[i]

        h = h + 0.5 * swiglu(rmsnorm(h, ffn2_norm[i]), ffn2_w_gate[i], ffn2_w_up[i], ffn2_w_down[i])

        gate = jax.nn.sigmoid(rmsnorm(h, ple_norm[i]) @ w_ple_gate[i])
        h = h + gate * (p[i] @ w_ple_proj[i])
    return rmsnorm(h, final_norm)
```

```python
import functools
import math

import jax
import jax.numpy as jnp
from jax import lax
from jax.experimental import pallas as pl
from jax.experimental.pallas import tpu as pltpu

PLE_DIM = 256
D_FF = 2816
CONV_K = 3
HEAD_DIM = 64
V_HEAD_DIM = 2 * HEAD_DIM
ROPE_THETA = 10000.0
EPS = 1e-6

LANES = 128
ROW_TILE = 256
ATTN_TILE = 256
FF_CHUNKS = ((0, 1536), (1536, D_FF))
VMEM_LIMIT_BYTES = 56 * 1024 * 1024
NEG_BIG = -1e30

_BF16 = jnp.bfloat16
_F32 = jnp.float32


def _rms(x, g):
    ms = jnp.mean(x * x, axis=-1, keepdims=True)
    return x * lax.rsqrt(ms + EPS) * g


def _dot(a, b):
    return jnp.dot(a, b, preferred_element_type=_F32)


def _swiglu(n, wg_ref, wu_ref, wd_ref):
    out = None
    for lo, hi in FF_CHUNKS:
        g = _dot(n, wg_ref[:, lo:hi])
        u = _dot(n, wu_ref[:, lo:hi])
        a = (g * jax.nn.sigmoid(g) * u).astype(_BF16)
        part = _dot(a, wd_ref[lo:hi, :])
        out = part if out is None else out + part
    return out


def _rope(x, cos, sin_signed, first_half):
    rot = jnp.where(first_half,
                    pltpu.roll(x, LANES - HEAD_DIM // 2, axis=1),
                    pltpu.roll(x, HEAD_DIM // 2, axis=1))
    return x * cos + rot * sin_signed


def _ffn_in_kernel(x_ref, g1_ref, wg_ref, wu_ref, wd_ref, gm_ref, win_ref, cw_ref,
                   cos_ref, sin_ref,
                   h_ref, yc_ref, q_ref, k_ref, vt_ref,
                   ubuf_ref, *, tiles_per_seq, conv_width, attn_width):
    tm = x_ref.shape[0]
    i = pl.program_id(0)

    x = x_ref[...]
    n1 = _rms(x, g1_ref[...]).astype(_BF16)
    h = x + 0.5 * _swiglu(n1, wg_ref, wu_ref, wd_ref)
    h_ref[...] = h

    n2 = _rms(h, gm_ref[...]).astype(_BF16)

    cw = conv_width
    c_b = _dot(n2, win_ref[:, 0:cw])
    c_c = _dot(n2, win_ref[:, cw:2 * cw])
    c_x = _dot(n2, win_ref[:, 2 * cw:3 * cw])
    u = c_c * c_x

    @pl.when(i % tiles_per_seq == 0)
    def _():
        ubuf_ref[0:8, :] = jnp.zeros((8, cw), _F32)

    ubuf_ref[8:8 + tm, :] = u
    u1 = ubuf_ref[7:7 + tm, :]
    u2 = ubuf_ref[6:6 + tm, :]
    y = c_b * (cw_ref[0:1, :] * u2 + cw_ref[1:2, :] * u1 + cw_ref[2:3, :] * u)
    yc_ref[...] = y.astype(_BF16)
    ubuf_ref[0:8, :] = ubuf_ref[tm:tm + 8, :]

    aw = attn_width
    lane = lax.broadcasted_iota(jnp.int32, (tm, LANES), 1)
    first_half = (lane % HEAD_DIM) < (HEAD_DIM // 2)
    cos = cos_ref[...]
    sin_signed = sin_ref[...]
    scale = HEAD_DIM ** -0.5
    q = _dot(n2, win_ref[:, 3 * cw:3 * cw + aw])
    k = _dot(n2, win_ref[:, 3 * cw + aw:3 * cw + 2 * aw])
    for c in range(aw // LANES):
        sl = slice(c * LANES, (c + 1) * LANES)
        q_ref[:, sl] = (_rope(q[:, sl], cos, sin_signed, first_half) * scale).astype(_BF16)
        k_ref[:, sl] = _rope(k[:, sl], cos, sin_signed, first_half).astype(_BF16)
    v = _dot(n2, win_ref[:, 3 * cw + 2 * aw:3 * cw + 3 * aw])
    vt = v.T.astype(_BF16)
    tk = vt_ref.shape[-1]
    for c in range(tm // tk):
        vt_ref[0, c] = vt[:, c * tk:(c + 1) * tk]


def _attn_kernel(q_ref, k_ref, vt_ref, lq1_ref, lk1_ref, lq2_ref, lk2_ref, g_ref,
                 o_ref, m_ref, l_ref, acc_ref, *, lam_init):
    t = q_ref.shape[0]
    qi = pl.program_id(2)

    q = q_ref[...]
    lane = lax.broadcasted_iota(jnp.int32, q.shape, 1)
    zero = jnp.zeros_like(q)
    qq = jnp.concatenate([jnp.where(lane < HEAD_DIM, q, zero),
                          jnp.where(lane >= HEAD_DIM, q, zero)], axis=0)

    m_ref[...] = jnp.full(m_ref.shape, NEG_BIG, _F32)
    l_ref[...] = jnp.zeros(l_ref.shape, _F32)
    acc_ref[...] = jnp.zeros(acc_ref.shape, _F32)

    def step(j, masked):
        kj = k_ref[pl.ds(pl.multiple_of(j * t, t), t), :]
        s = lax.dot_general(kj, qq, (((1,), (1,)), ((), ())), preferred_element_type=_F32)
        if masked:
            key = lax.broadcasted_iota(jnp.int32, s.shape, 0)
            qry = lax.broadcasted_iota(jnp.int32, s.shape, 1) % t
            s = jnp.where(key <= qry, s, NEG_BIG)
        m_prev = m_ref[...]
        m_new = jnp.maximum(m_prev, jnp.max(s, axis=0, keepdims=True))
        alpha = jnp.exp(m_prev - m_new)
        p = jnp.exp(s - m_new)
        l_ref[...] = alpha * l_ref[...] + jnp.sum(p, axis=0, keepdims=True)
        m_ref[...] = m_new
        pv = _dot(vt_ref[0, j], p.astype(_BF16))
        acc_ref[...] = acc_ref[...] * alpha + pv

    def body(j, carry):
        step(j, False)
        return carry

    lax.fori_loop(0, qi, body, 0)
    step(qi, True)

    lam = (jnp.exp(jnp.sum(lq1_ref[...] * lk1_ref[...], keepdims=True))
           - jnp.exp(jnp.sum(lq2_ref[...] * lk2_ref[...], keepdims=True)) + lam_init)
    o = acc_ref[...] * (1.0 / l_ref[...])
    d = o[:, :t] - lam * o[:, t:]
    ms = jnp.mean(d * d, axis=0, keepdims=True)
    d = d * lax.rsqrt(ms + EPS) * g_ref[...] * (1.0 - lam_init)
    o_ref[...] = d.T.astype(o_ref.dtype)


def _out_ffn_kernel(h_ref, yc_ref, ya_ref, p_ref, wo_ref, g2_ref, wg_ref, wu_ref, wd_ref,
                    gp_ref, wpg_ref, wpp_ref, gf_ref, o_ref, *, conv_width, final_norm):
    cw = conv_width
    h = (h_ref[...] + _dot(yc_ref[...], wo_ref[0:cw, :])
         + _dot(ya_ref[...], wo_ref[cw:, :]))
    n = _rms(h, g2_ref[...]).astype(_BF16)
    h = h + 0.5 * _swiglu(n, wg_ref, wu_ref, wd_ref)
    n = _rms(h, gp_ref[...]).astype(_BF16)
    gate = jax.nn.sigmoid(_dot(n, wpg_ref[...]))
    h = h + gate * _dot(p_ref[...].astype(_BF16), wpp_ref[...])
    if final_norm:
        h = _rms(h, gf_ref[...])
    o_ref[...] = h


def _resident(shape):
    return pl.BlockSpec(shape, lambda *_: (0,) * len(shape), pipeline_mode=pl.Buffered(1))


def _rope_tables(s):
    inv_freq = 1.0 / (ROPE_THETA ** (jnp.arange(0, HEAD_DIM, 2, dtype=_F32) / HEAD_DIM))
    ang = jnp.arange(s, dtype=_F32)[:, None] * inv_freq[None, :]
    reps = LANES // (HEAD_DIM // 2)
    cos = jnp.tile(jnp.cos(ang), (1, reps))
    sign = jnp.where((jnp.arange(LANES) % HEAD_DIM) < HEAD_DIM // 2, -1.0, 1.0).astype(_F32)
    sin_signed = jnp.tile(jnp.sin(ang), (1, reps)) * sign[None, :]
    return cos, sin_signed


def _layer(h, p, cos, sin_signed, w, lam_init, final_gain, *, batch, seq):
    rows, d_model = h.shape
    conv_width = w["conv_w"].shape[-1]
    attn_width = (w["w_in"].shape[-1] - 3 * conv_width) // 3
    n_heads = attn_width // V_HEAD_DIM
    tm, t = ROW_TILE, ATTN_TILE
    n_row_tiles = rows // tm
    tiles_per_seq = seq // tm
    nq = seq // t
    row = lambda i: (i, 0)
    vec = lambda a: a.reshape(1, -1)
    dense_params = pltpu.CompilerParams(dimension_semantics=("arbitrary",),
                                        vmem_limit_bytes=VMEM_LIMIT_BYTES)

    h1, y_conv, q, k, vt = pl.pallas_call(
        functools.partial(_ffn_in_kernel, tiles_per_seq=tiles_per_seq,
                          conv_width=conv_width, attn_width=attn_width),
        grid=(n_row_tiles,),
        in_specs=[
            pl.BlockSpec((tm, d_model), row),
            _resident((1, d_model)),
            _resident((d_model, D_FF)),
            _resident((d_model, D_FF)),
            _resident((D_FF, d_model)),
            _resident((1, d_model)),
            _resident(w["w_in"].shape),
            _resident((CONV_K, conv_width)),
            pl.BlockSpec((tm, LANES), lambda i: (i % tiles_per_seq, 0)),
            pl.BlockSpec((tm, LANES), lambda i: (i % tiles_per_seq, 0)),
        ],
        out_specs=[
            pl.BlockSpec((tm, d_model), row),
            pl.BlockSpec((tm, conv_width), row),
            pl.BlockSpec((tm, attn_width), row),
            pl.BlockSpec((tm, attn_width), row),
            pl.BlockSpec((1, tm // t, attn_width, t),
                         lambda i: (i // tiles_per_seq, i % tiles_per_seq, 0, 0)),
        ],
        out_shape=[
            jax.ShapeDtypeStruct((rows, d_model), _F32),
            jax.ShapeDtypeStruct((rows, conv_width), _BF16),
            jax.ShapeDtypeStruct((rows, attn_width), _BF16),
            jax.ShapeDtypeStruct((rows, attn_width), _BF16),
            jax.ShapeDtypeStruct((batch, nq, attn_width, t), _BF16),
        ],
        scratch_shapes=[pltpu.VMEM((tm + 8, conv_width), _F32)],
        compiler_params=dense_params,
        name="ffn1_inproj",
    )(h, vec(w["ffn1_norm"]), w["ffn1_w_gate"], w["ffn1_w_up"], w["ffn1_w_down"],
      vec(w["mix_norm"]), w["w_in"], w["conv_w"], cos, sin_signed)

    lam_spec = pl.BlockSpec((1, HEAD_DIM), lambda b, hh, i: (0, 0))
    y_attn = pl.pallas_call(
        functools.partial(_attn_kernel, lam_init=lam_init),
        grid=(batch, n_heads, nq),
        in_specs=[
            pl.BlockSpec((t, V_HEAD_DIM), lambda b, hh, i: (b * nq + i, hh)),
            pl.BlockSpec((seq, V_HEAD_DIM), lambda b, hh, i: (b, hh)),
            pl.BlockSpec((1, nq, V_HEAD_DIM, t), lambda b, hh, i: (b, 0, hh, 0)),
            lam_spec, lam_spec, lam_spec, lam_spec,
            pl.BlockSpec((V_HEAD_DIM, 1), lambda b, hh, i: (0, 0)),
        ],
        out_specs=pl.BlockSpec((t, V_HEAD_DIM), lambda b, hh, i: (b * nq + i, hh)),
        out_shape=jax.ShapeDtypeStruct((rows, attn_width), _BF16),
        scratch_shapes=[pltpu.VMEM((1, 2 * t), _F32),
                        pltpu.VMEM((1, 2 * t), _F32),
                        pltpu.VMEM((V_HEAD_DIM, 2 * t), _F32)],
        compiler_params=pltpu.CompilerParams(
            dimension_semantics=("arbitrary", "arbitrary", "arbitrary"),
            vmem_limit_bytes=VMEM_LIMIT_BYTES),
        name="diff_attn",
    )(q, k, vt, vec(w["lambda_q1"]), vec(w["lambda_k1"]), vec(w["lambda_q2"]),
      vec(w["lambda_k2"]), w["subln_g"].reshape(-1, 1))

    return pl.pallas_call(
        functools.partial(_out_ffn_kernel, conv_width=conv_width,
                          final_norm=final_gain is not None),
        grid=(n_row_tiles,),
        in_specs=[
            pl.BlockSpec((tm, d_model), row),
            pl.BlockSpec((tm, conv_width), row),
            pl.BlockSpec((tm, attn_width), row),
            pl.BlockSpec((tm, PLE_DIM), row),
            _resident(w["w_out"].shape),
            _resident((1, d_model)),
            _resident((d_model, D_FF)),
            _resident((d_model, D_FF)),
            _resident((D_FF, d_model)),
            _resident((1, d_model)),
            _resident((d_model, d_model)),
            _resident((PLE_DIM, d_model)),
            _resident((1, d_model)),
        ],
        out_specs=pl.BlockSpec((tm, d_model), row),
        out_shape=jax.ShapeDtypeStruct((rows, d_model), _F32),
        compiler_params=dense_params,
        name="outproj_ffn2_ple",
    )(h1, y_conv, y_attn, p, w["w_out"], vec(w["ffn2_norm"]), w["ffn2_w_gate"],
      w["ffn2_w_up"], w["ffn2_w_down"], vec(w["ple_norm"]), w["w_ple_gate"],
      w["w_ple_proj"], vec(final_gain if final_gain is not None else w["ple_norm"]))


_MATMUL_WEIGHTS = ("ffn1_w_gate", "ffn1_w_up", "ffn1_w_down", "w_in", "w_out",
                   "ffn2_w_gate", "ffn2_w_up", "ffn2_w_down", "w_ple_gate", "w_ple_proj")


def kernel(x, p, ffn1_norm, ffn1_w_gate, ffn1_w_up, ffn1_w_down, mix_norm, w_in, conv_w,
           lambda_q1, lambda_k1, lambda_q2, lambda_k2, subln_g, w_out,
           ffn2_norm, ffn2_w_gate, ffn2_w_up, ffn2_w_down,
           ple_norm, w_ple_gate, w_ple_proj, final_norm):
    batch, seq, d_model = x.shape
    depth = p.shape[0]
    assert seq % ROW_TILE == 0 and seq % ATTN_TILE == 0 and ROW_TILE % ATTN_TILE == 0
    stacked = dict(
        ffn1_norm=ffn1_norm, ffn1_w_gate=ffn1_w_gate, ffn1_w_up=ffn1_w_up,
        ffn1_w_down=ffn1_w_down, mix_norm=mix_norm, w_in=w_in, conv_w=conv_w,
        lambda_q1=lambda_q1, lambda_k1=lambda_k1, lambda_q2=lambda_q2, lambda_k2=lambda_k2,
        subln_g=subln_g, w_out=w_out, ffn2_norm=ffn2_norm, ffn2_w_gate=ffn2_w_gate,
        ffn2_w_up=ffn2_w_up, ffn2_w_down=ffn2_w_down, ple_norm=ple_norm,
        w_ple_gate=w_ple_gate, w_ple_proj=w_ple_proj)
    cos, sin_signed = _rope_tables(seq)
    h = x.reshape(batch * seq, d_model)
    for i in range(depth):
        w = {name: (a[i].astype(_BF16) if name in _MATMUL_WEIGHTS else a[i])
             for name, a in stacked.items()}
        lam_init = 0.8 - 0.6 * math.exp(-0.3 * i)
        h = _layer(h, p[i].reshape(batch * seq, -1), cos, sin_signed, w, lam_init,
                   final_norm if i == depth - 1 else None, batch=batch, seq=seq)
    return h.reshape(batch, seq, d_model)
```

```python
import functools
import math

import jax
import jax.numpy as jnp
from jax import lax
from jax.experimental import pallas as pl
from jax.experimental.pallas import tpu as pltpu

PLE_DIM = 256
D_FF = 2816
CONV_K = 3
HEAD_DIM = 64
V_HEAD_DIM = 2 * HEAD_DIM
ROPE_THETA = 10000.0
EPS = 1e-6

LANES = 128
ROW_TILE = 256
ATTN_TILE = 256
FF_CHUNKS = ((0, 1536), (1536, D_FF))
VMEM_LIMIT_BYTES = 56 * 1024 * 1024
SUM_ROWS = 16
NEG_BIG = -1e30

_BF16 = jnp.bfloat16
_F32 = jnp.float32


def _rms(x, g):
    ms = jnp.mean(x * x, axis=-1, keepdims=True)
    return x * lax.rsqrt(ms + EPS) * g


def _dot(a, b):
    return jnp.dot(a, b, preferred_element_type=_F32)


def _swiglu(n, wg_ref, wu_ref, wd_ref):
    out = None
    for lo, hi in FF_CHUNKS:
        g = _dot(n, wg_ref[:, lo:hi])
        u = _dot(n, wu_ref[:, lo:hi])
        a = (g * jax.nn.sigmoid(g) * u).astype(_BF16)
        part = _dot(a, wd_ref[lo:hi, :])
        out = part if out is None else out + part
    return out


def _rope(x, cos, sin_signed, first_half):
    rot = jnp.where(first_half,
                    pltpu.roll(x, LANES - HEAD_DIM // 2, axis=1),
                    pltpu.roll(x, HEAD_DIM // 2, axis=1))
    return x * cos + rot * sin_signed


def _ffn_in_kernel(x_ref, g1_ref, wg_ref, wu_ref, wd_ref, gm_ref, win_ref, cw_ref,
                   cos_ref, sin_ref,
                   h_ref, yc_ref, q_ref, k_ref, vt_ref,
                   ubuf_ref, *, tiles_per_seq, conv_width, attn_width):
    tm = x_ref.shape[0]
    i = pl.program_id(0)

    x = x_ref[...]
    n1 = _rms(x, g1_ref[...]).astype(_BF16)
    h = x + 0.5 * _swiglu(n1, wg_ref, wu_ref, wd_ref)
    h_ref[...] = h

    n2 = _rms(h, gm_ref[...]).astype(_BF16)

    cw = conv_width
    c_b = _dot(n2, win_ref[:, 0:cw])
    c_c = _dot(n2, win_ref[:, cw:2 * cw])
    c_x = _dot(n2, win_ref[:, 2 * cw:3 * cw])
    u = c_c * c_x

    @pl.when(i % tiles_per_seq == 0)
    def _():
        ubuf_ref[0:8, :] = jnp.zeros((8, cw), _F32)

    ubuf_ref[8:8 + tm, :] = u
    u1 = ubuf_ref[7:7 + tm, :]
    u2 = ubuf_ref[6:6 + tm, :]
    y = c_b * (cw_ref[0:1, :] * u2 + cw_ref[1:2, :] * u1 + cw_ref[2:3, :] * u)
    yc_ref[...] = y.astype(_BF16)
    ubuf_ref[0:8, :] = ubuf_ref[tm:tm + 8, :]

    aw = attn_width
    lane = lax.broadcasted_iota(jnp.int32, (tm, LANES), 1)
    first_half = (lane % HEAD_DIM) < (HEAD_DIM // 2)
    cos = cos_ref[...]
    sin_signed = sin_ref[...]
    scale = HEAD_DIM ** -0.5 * math.log2(math.e)
    q = _dot(n2, win_ref[:, 3 * cw:3 * cw + aw])
    k = _dot(n2, win_ref[:, 3 * cw + aw:3 * cw + 2 * aw])
    for c in range(aw // LANES):
        sl = slice(c * LANES, (c + 1) * LANES)
        q_ref[:, sl] = (_rope(q[:, sl], cos, sin_signed, first_half) * scale).astype(_BF16)
        k_ref[:, sl] = _rope(k[:, sl], cos, sin_signed, first_half).astype(_BF16)
    v = _dot(n2, win_ref[:, 3 * cw + 2 * aw:3 * cw + 3 * aw])
    vt = v.T.astype(_BF16)
    tk = vt_ref.shape[-1]
    for c in range(tm // tk):
        vt_ref[0, c] = vt[:, c * tk:(c + 1) * tk]


def _attn_kernel(q_ref, k_ref, vt_ref, lq1_ref, lk1_ref, lq2_ref, lk2_ref, g_ref,
                 o_ref, qq_ref, s_ref, cmax_ref, p_ref, alpha_ref, m_ref, acc_ref,
                 *, lam_init, n_heads):
    t = q_ref.shape[0]
    dv = V_HEAD_DIM
    n_streams = 2 * n_heads
    qi = pl.program_id(1)

    for h in range(n_heads):
        q = q_ref[:, h * dv:(h + 1) * dv]
        lane = lax.broadcasted_iota(jnp.int32, q.shape, 1)
        zero = jnp.zeros_like(q)
        qq_ref[2 * h] = jnp.where(lane < HEAD_DIM, q, zero)
        qq_ref[2 * h + 1] = jnp.where(lane >= HEAD_DIM, q, zero)
    m_ref[...] = jnp.full(m_ref.shape, NEG_BIG, _F32)
    acc_ref[...] = jnp.zeros(acc_ref.shape, _F32)
    p_ref[1] = jnp.zeros(p_ref.shape[1:], _BF16)
    alpha_ref[1] = jnp.ones(alpha_ref.shape[1:], _F32)
    ones = jnp.ones((SUM_ROWS, t), _BF16)

    def scores(j, slot, u):
        h = u // 2
        kj = k_ref[pl.ds(pl.multiple_of(j * t, t), t), h * dv:(h + 1) * dv]
        s = lax.dot_general(kj, qq_ref[u], (((1,), (1,)), ((), ())),
                            preferred_element_type=_F32)
        s_ref[slot, u] = s
        cmax_ref[slot, u] = jnp.max(s, axis=0, keepdims=True)

    def softmax(slot, u, masked):
        s = s_ref[slot, u]
        if masked:
            key = lax.broadcasted_iota(jnp.int32, s.shape, 0)
            qry = lax.broadcasted_iota(jnp.int32, s.shape, 1)
            s = jnp.where(key <= qry, s, NEG_BIG)
            cmax = jnp.max(s, axis=0, keepdims=True)
        else:
            cmax = cmax_ref[slot, u]
        m_prev = m_ref[u]
        m_new = jnp.maximum(m_prev, cmax)
        alpha_ref[slot, u] = jnp.exp2(m_prev - m_new)
        p_ref[slot, u] = jnp.exp2(s - m_new).astype(_BF16)
        m_ref[u] = m_new

    def accumulate(j, slot, u):
        h = u // 2
        v_ones = jnp.concatenate([vt_ref[0, j, h * dv:(h + 1) * dv, :], ones], axis=0)
        acc_ref[u] = (acc_ref[u] * alpha_ref[slot, u]
                      + _dot(v_ones, p_ref[slot, u]))

    def iteration(j, slot):
        for u in range(n_streams):
            accumulate(jnp.maximum(j - 1, 0), 1 - slot, u)
            softmax(slot, u, False)
            scores(j + 1, 1 - slot, u)

    def finish(slot):
        for u in range(n_streams):
            accumulate(jnp.maximum(qi - 1, 0), 1 - slot, u)
            softmax(slot, u, True)
            accumulate(qi, slot, u)

    for u in range(n_streams):
        scores(0, 0, u)

    def pair(i, carry):
        iteration(2 * i, 0)
        iteration(2 * i + 1, 1)
        return carry

    lax.fori_loop(0, qi // 2, pair, 0)

    @pl.when(qi % 2 == 1)
    def _():
        iteration(qi - 1, 0)
        finish(1)

    @pl.when(qi % 2 == 0)
    def _():
        finish(0)

    lam = (jnp.exp(jnp.sum(lq1_ref[...] * lk1_ref[...], keepdims=True))
           - jnp.exp(jnp.sum(lq2_ref[...] * lk2_ref[...], keepdims=True)) + lam_init)
    for h in range(n_heads):
        o1 = acc_ref[2 * h, 0:dv, :] * (1.0 / acc_ref[2 * h, dv:dv + 1, :])
        o2 = acc_ref[2 * h + 1, 0:dv, :] * (1.0 / acc_ref[2 * h + 1, dv:dv + 1, :])
        d = o1 - lam * o2
        ms = jnp.mean(d * d, axis=0, keepdims=True)
        d = d * lax.rsqrt(ms + EPS) * g_ref[...] * (1.0 - lam_init)
        o_ref[:, h * dv:(h + 1) * dv] = d.T.astype(o_ref.dtype)


def _out_ffn_kernel(h_ref, yc_ref, ya_ref, p_ref, wo_ref, g2_ref, wg_ref, wu_ref, wd_ref,
                    gp_ref, wpg_ref, wpp_ref, gf_ref, o_ref, *, conv_width, final_norm):
    cw = conv_width
    h = (h_ref[...] + _dot(yc_ref[...], wo_ref[0:cw, :])
         + _dot(ya_ref[...], wo_ref[cw:, :]))
    n = _rms(h, g2_ref[...]).astype(_BF16)
    h = h + 0.5 * _swiglu(n, wg_ref, wu_ref, wd_ref)
    n = _rms(h, gp_ref[...]).astype(_BF16)
    gate = jax.nn.sigmoid(_dot(n, wpg_ref[...]))
    h = h + gate * _dot(p_ref[...].astype(_BF16), wpp_ref[...])
    if final_norm:
        h = _rms(h, gf_ref[...])
    o_ref[...] = h


def _resident(shape):
    return pl.BlockSpec(shape, lambda *_: (0,) * len(shape), pipeline_mode=pl.Buffered(1))


def _rope_tables(s):
    inv_freq = 1.0 / (ROPE_THETA ** (jnp.arange(0, HEAD_DIM, 2, dtype=_F32) / HEAD_DIM))
    ang = jnp.arange(s, dtype=_F32)[:, None] * inv_freq[None, :]
    reps = LANES // (HEAD_DIM // 2)
    cos = jnp.tile(jnp.cos(ang), (1, reps))
    sign = jnp.where((jnp.arange(LANES) % HEAD_DIM) < HEAD_DIM // 2, -1.0, 1.0).astype(_F32)
    sin_signed = jnp.tile(jnp.sin(ang), (1, reps)) * sign[None, :]
    return cos, sin_signed


def _layer(h, p, cos, sin_signed, w, lam_init, final_gain, *, batch, seq):
    rows, d_model = h.shape
    conv_width = w["conv_w"].shape[-1]
    attn_width = (w["w_in"].shape[-1] - 3 * conv_width) // 3
    n_heads = attn_width // V_HEAD_DIM
    tm, t = ROW_TILE, ATTN_TILE
    n_row_tiles = rows // tm
    tiles_per_seq = seq // tm
    nq = seq // t
    row = lambda i: (i, 0)
    vec = lambda a: a.reshape(1, -1)
    dense_params = pltpu.CompilerParams(dimension_semantics=("arbitrary",),
                                        vmem_limit_bytes=VMEM_LIMIT_BYTES)

    h1, y_conv, q, k, vt = pl.pallas_call(
        functools.partial(_ffn_in_kernel, tiles_per_seq=tiles_per_seq,
                          conv_width=conv_width, attn_width=attn_width),
        grid=(n_row_tiles,),
        in_specs=[
            pl.BlockSpec((tm, d_model), row),
            _resident((1, d_model)),
            _resident((d_model, D_FF)),
            _resident((d_model, D_FF)),
            _resident((D_FF, d_model)),
            _resident((1, d_model)),
            _resident(w["w_in"].shape),
            _resident((CONV_K, conv_width)),
            pl.BlockSpec((tm, LANES), lambda i: (i % tiles_per_seq, 0)),
            pl.BlockSpec((tm, LANES), lambda i: (i % tiles_per_seq, 0)),
        ],
        out_specs=[
            pl.BlockSpec((tm, d_model), row),
            pl.BlockSpec((tm, conv_width), row),
            pl.BlockSpec((tm, attn_width), row),
            pl.BlockSpec((tm, attn_width), row),
            pl.BlockSpec((1, tm // t, attn_width, t),
                         lambda i: (i // tiles_per_seq, i % tiles_per_seq, 0, 0)),
        ],
        out_shape=[
            jax.ShapeDtypeStruct((rows, d_model), _F32),
            jax.ShapeDtypeStruct((rows, conv_width), _BF16),
            jax.ShapeDtypeStruct((rows, attn_width), _BF16),
            jax.ShapeDtypeStruct((rows, attn_width), _BF16),
            jax.ShapeDtypeStruct((batch, nq, attn_width, t), _BF16),
        ],
        scratch_shapes=[pltpu.VMEM((tm + 8, conv_width), _F32)],
        compiler_params=dense_params,
        name="ffn1_inproj",
    )(h, vec(w["ffn1_norm"]), w["ffn1_w_gate"], w["ffn1_w_up"], w["ffn1_w_down"],
      vec(w["mix_norm"]), w["w_in"], w["conv_w"], cos, sin_signed)

    lam_spec = pl.BlockSpec((1, HEAD_DIM), lambda b, i: (0, 0))
    y_attn = pl.pallas_call(
        functools.partial(_attn_kernel, lam_init=lam_init, n_heads=n_heads),
        grid=(batch, nq),
        in_specs=[
            pl.BlockSpec((t, attn_width), lambda b, i: (b * nq + i, 0)),
            pl.BlockSpec((seq, attn_width), lambda b, i: (b, 0), pipeline_mode=pl.Buffered(1)),
            pl.BlockSpec((1, nq, attn_width, t), lambda b, i: (b, 0, 0, 0),
                         pipeline_mode=pl.Buffered(1)),
            lam_spec, lam_spec, lam_spec, lam_spec,
            pl.BlockSpec((V_HEAD_DIM, 1), lambda b, i: (0, 0)),
        ],
        out_specs=pl.BlockSpec((t, attn_width), lambda b, i: (b * nq + i, 0)),
        out_shape=jax.ShapeDtypeStruct((rows, attn_width), _BF16),
        scratch_shapes=[pltpu.VMEM((2 * n_heads, t, V_HEAD_DIM), _BF16),
                        pltpu.VMEM((2, 2 * n_heads, t, t), _F32),
                        pltpu.VMEM((2, 2 * n_heads, 1, t), _F32),
                        pltpu.VMEM((2, 2 * n_heads, t, t), _BF16),
                        pltpu.VMEM((2, 2 * n_heads, 1, t), _F32),
                        pltpu.VMEM((2 * n_heads, 1, t), _F32),
                        pltpu.VMEM((2 * n_heads, V_HEAD_DIM + SUM_ROWS, t), _F32)],
        compiler_params=pltpu.CompilerParams(
            dimension_semantics=("arbitrary", "arbitrary"),
            vmem_limit_bytes=VMEM_LIMIT_BYTES),
        name="diff_attn",
    )(q, k, vt, vec(w["lambda_q1"]), vec(w["lambda_k1"]), vec(w["lambda_q2"]),
      vec(w["lambda_k2"]), w["subln_g"].reshape(-1, 1))

    return pl.pallas_call(
        functools.partial(_out_ffn_kernel, conv_width=conv_width,
                          final_norm=final_gain is not None),
        grid=(n_row_tiles,),
        in_specs=[
            pl.BlockSpec((tm, d_model), row),
            pl.BlockSpec((tm, conv_width), row),
            pl.BlockSpec((tm, attn_width), row),
            pl.BlockSpec((tm, PLE_DIM), row),
            _resident(w["w_out"].shape),
            _resident((1, d_model)),
            _resident((d_model, D_FF)),
            _resident((d_model, D_FF)),
            _resident((D_FF, d_model)),
            _resident((1, d_model)),
            _resident((d_model, d_model)),
            _resident((PLE_DIM, d_model)),
            _resident((1, d_model)),
        ],
        out_specs=pl.BlockSpec((tm, d_model), row),
        out_shape=jax.ShapeDtypeStruct((rows, d_model), _F32),
        compiler_params=dense_params,
        name="outproj_ffn2_ple",
    )(h1, y_conv, y_attn, p, w["w_out"], vec(w["ffn2_norm"]), w["ffn2_w_gate"],
      w["ffn2_w_up"], w["ffn2_w_down"], vec(w["ple_norm"]), w["w_ple_gate"],
      w["w_ple_proj"], vec(final_gain if final_gain is not None else w["ple_norm"]))


_MATMUL_WEIGHTS = ("ffn1_w_gate", "ffn1_w_up", "ffn1_w_down", "w_in", "w_out",
                   "ffn2_w_gate", "ffn2_w_up", "ffn2_w_down", "w_ple_gate", "w_ple_proj")


def kernel(x, p, ffn1_norm, ffn1_w_gate, ffn1_w_up, ffn1_w_down, mix_norm, w_in, conv_w,
           lambda_q1, lambda_k1, lambda_q2, lambda_k2, subln_g, w_out,
           ffn2_norm, ffn2_w_gate, ffn2_w_up, ffn2_w_down,
           ple_norm, w_ple_gate, w_ple_proj, final_norm):
    batch, seq, d_model = x.shape
    depth = p.shape[0]
    assert seq % ROW_TILE == 0 and seq % ATTN_TILE == 0 and ROW_TILE % ATTN_TILE == 0
    stacked = dict(
        ffn1_norm=ffn1_norm, ffn1_w_gate=ffn1_w_gate, ffn1_w_up=ffn1_w_up,
        ffn1_w_down=ffn1_w_down, mix_norm=mix_norm, w_in=w_in, conv_w=conv_w,
        lambda_q1=lambda_q1, lambda_k1=lambda_k1, lambda_q2=lambda_q2, lambda_k2=lambda_k2,
        subln_g=subln_g, w_out=w_out, ffn2_norm=ffn2_norm, ffn2_w_gate=ffn2_w_gate,
        ffn2_w_up=ffn2_w_up, ffn2_w_down=ffn2_w_down, ple_norm=ple_norm,
        w_ple_gate=w_ple_gate, w_ple_proj=w_ple_proj)
    cos, sin_signed = _rope_tables(seq)
    h = x.reshape(batch * seq, d_model)
    for i in range(depth):
        w = {name: (a[i].astype(_BF16) if name in _MATMUL_WEIGHTS else a[i])
             for name, a in stacked.items()}
        lam_init = 0.8 - 0.6 * math.exp(-0.3 * i)
        h = _layer(h, p[i].reshape(batch * seq, -1), cos, sin_signed, w, lam_init,
                   final_norm if i == depth - 1 else None, batch=batch, seq=seq)
    return h.reshape(batch, seq, d_model)
```
